```python
import jax, jax.numpy as jnp
from jax import lax
import numpy as np

D_MODEL = 1024
BATCH = 32
SEQ = 256
DEPTH = 2
DEC_BATCH = 8
DEC_SEQ = 4096
PAST_LEN = 512

GRID_W = 64
HEAD_DIM = 64
ATTN_WIDTH = D_MODEL // 2
ATTN_HEADS = ATTN_WIDTH // HEAD_DIM
ATTN_KV_HEADS = ATTN_HEADS // 4
ATTN_GROUP = ATTN_HEADS // ATTN_KV_HEADS
KV_WIDTH = ATTN_KV_HEADS * HEAD_DIM
HG_WIDTH = D_MODEL // 4
HG_DK = 64
HG_DV = 64
HG_HEADS = HG_WIDTH // HG_DK
FT_WIDTH = D_MODEL // 4
FT_CH = 64
FT_GROUPS = FT_WIDTH // FT_CH
MIX_WIDTH = ATTN_WIDTH + HG_WIDTH + FT_WIDTH
IN_WIDTH = ATTN_WIDTH + 2 * KV_WIDTH + 5 * HG_WIDTH + FT_WIDTH
IN_SPLITS = (
    ATTN_WIDTH,
    ATTN_WIDTH + KV_WIDTH,
    ATTN_WIDTH + 2 * KV_WIDTH,
    ATTN_WIDTH + 2 * KV_WIDTH + HG_WIDTH,
    ATTN_WIDTH + 2 * KV_WIDTH + 2 * HG_WIDTH,
    ATTN_WIDTH + 2 * KV_WIDTH + 3 * HG_WIDTH,
    ATTN_WIDTH + 2 * KV_WIDTH + 4 * HG_WIDTH,
    ATTN_WIDTH + 2 * KV_WIDTH + 5 * HG_WIDTH,
)
N_GROUPS = 4
EXPERTS_PER_GROUP = 4
N_EXPERTS = N_GROUPS * EXPERTS_PER_GROUP
MOE_TOP_K = 2
D_EXPERT = 256
Q_BLOCK = 128
CHUNK = 64
ROPE_THETA = 10000.0
EPS = 1e-6
EXP_CLIP = 80.0

kernel_name = 'hybrid_prefix_diffusion_step'

F32 = jnp.float32


def _rms(x, w):
    xf = x.astype(F32)
    y = xf * lax.rsqrt(jnp.mean(xf * xf, axis=-1, keepdims=True) + EPS)
    return (y * w.astype(F32)).astype(x.dtype)


def _modulation(cond, w_ada, b_ada):
    m = jnp.einsum('bd,de->be', jax.nn.silu(cond), w_ada) + b_ada
    return tuple(jnp.split(m[:, None, :], 6, axis=-1))


def _axial_angles(n_tokens):
    rows = n_tokens // GRID_W
    row = jnp.repeat(jnp.arange(rows, dtype=F32), GRID_W)
    col = jnp.tile(jnp.arange(GRID_W, dtype=F32), rows)
    half = HEAD_DIM // 2
    inv = 1.0 / (ROPE_THETA ** (jnp.arange(0, half, 2, dtype=F32) / half))
    return row[:, None] * inv[None, :], col[:, None] * inv[None, :]


def _rotate_half(x, ang):
    cos = jnp.cos(ang)[None, :, None, :].astype(x.dtype)
    sin = jnp.sin(ang)[None, :, None, :].astype(x.dtype)
    x1, x2 = jnp.split(x, 2, axis=-1)
    return jnp.concatenate([x1 * cos - x2 * sin, x2 * cos + x1 * sin], axis=-1)


def _axial_rope(x, ang_row, ang_col):
    half = HEAD_DIM // 2
    return jnp.concatenate([_rotate_half(x[..., :half], ang_row), _rotate_half(x[..., half:], ang_col)], axis=-1)


def _attend(q, k, v):
    B, Lq = q.shape[0], q.shape[1]
    nb = Lq // Q_BLOCK
    qb = q.reshape(B, nb, Q_BLOCK, ATTN_KV_HEADS, ATTN_GROUP, HEAD_DIM).transpose(1, 0, 2, 3, 4, 5)
    scale = HEAD_DIM ** -0.5

    def one_block(qblk):
        s = jnp.einsum('bqkgd,bskd->bkgqs', qblk, k).astype(F32) * scale
        p = jax.nn.softmax(s, axis=-1).astype(v.dtype)
        return jnp.einsum('bkgqs,bskd->bqkgd', p, v)

    o = lax.map(one_block, qb)
    return o.transpose(1, 0, 2, 3, 4, 5).reshape(B, Lq, ATTN_WIDTH)


def _forget_gate(z, lb):
    z = z.astype(F32)
    logf = jax.nn.log_sigmoid(z) + jnp.log1p(lb * jnp.exp(jnp.minimum(-z, EXP_CLIP)))
    k = (1.0 - lb) * jax.nn.sigmoid(-z)
    shp = z.shape[:-1] + (HG_HEADS, HG_DK)
    return logf.reshape(shp), k.reshape(shp)


def _chunk_gla(q, k, v, logf, s0):
    B, L, H, DK = q.shape
    DV = v.shape[-1]
    n = L // CHUNK

    def chunks(a):
        return a.reshape(B, n, CHUNK, H, a.shape[-1]).transpose(1, 0, 3, 2, 4)

    lower = jnp.tril(jnp.ones((CHUNK, CHUNK), dtype=bool))[:, :, None]

    def step(S, inp):
        qc, kc, vc, gc = inp
        b = jnp.cumsum(gc, axis=2)
        o_inter = jnp.einsum('bhtk,bhkv->bhtv', qc * jnp.exp(b), S)
        diff = b[:, :, :, None, :] - b[:, :, None, :, :]
        rel = jnp.where(lower, jnp.exp(jnp.minimum(diff, 0.0)), 0.0)
        a = jnp.einsum('bhtk,bhsk,bhtsk->bhts', qc, kc, rel)
        o_intra = jnp.einsum('bhts,bhsv->bhtv', a, vc)
        b_end = b[:, :, -1:, :]
        S = jnp.exp(b_end[:, :, 0, :])[..., None] * S + jnp.einsum('bhsk,bhsv->bhkv', kc * jnp.exp(b_end - b), vc)
        return S, o_inter + o_intra

    S, o = lax.scan(step, s0, (chunks(q), chunks(k), chunks(v), chunks(logf)))
    return o.transpose(1, 0, 3, 2, 4).reshape(B, L, H, DV), S


def _hier_moe(h, w_gr, b_gr, w_er, b_er, w_gate, w_up, w_down):
    B, L, D = h.shape
    t = h.reshape(B * L, D)
    lg = (jnp.einsum('td,dg->tg', t, w_gr) + b_gr).astype(F32)
    pg = jax.nn.softmax(lg, axis=-1)
    g_idx = jnp.argmax(lg, axis=-1)
    g_w = jnp.take_along_axis(pg, g_idx[:, None], axis=-1)
    le = (jnp.einsum('td,de->te', t, w_er) + b_er).astype(F32).reshape(-1, N_GROUPS, EXPERTS_PER_GROUP)
    le_sel = jnp.take_along_axis(le, g_idx[:, None, None], axis=1)[:, 0]
    top_v, top_i = lax.top_k(le_sel, MOE_TOP_K)
    w2 = jax.nn.softmax(top_v, axis=-1) * g_w
    w_in_group = jnp.einsum('tk,tke->te', w2, jax.nn.one_hot(top_i, EXPERTS_PER_GROUP, dtype=F32))
    gate = (jax.nn.one_hot(g_idx, N_GROUPS, dtype=F32)[:, :, None] * w_in_group[:, None, :]).astype(h.dtype)
    out = jnp.zeros_like(t)
    for g in range(N_GROUPS):
        e0 = g * EXPERTS_PER_GROUP
        e1 = e0 + EXPERTS_PER_GROUP
        a = jnp.einsum('td,edf->tef', t, w_gate[e0:e1])
        u = jnp.einsum('td,edf->tef', t, w_up[e0:e1])
        hid = jax.nn.silu(a) * u * gate[:, g, :, None]
        out = out + jnp.einsum('tef,efd->td', hid, w_down[e0:e1])
    return out.reshape(B, L, D)


def _block(x, mods, angles, ctx_k, ctx_v, s_f0, s_b0, lb,
           norm1_w, norm2_w, w_in, q_norm_w, k_norm_w, hg_norm_w, w_ft, w_out,
           w_gr, b_gr, w_er, b_er, w_gate, w_up, w_down):
    shift1, scale1, gate1, shift2, scale2, gate2 = mods
    B, L, _ = x.shape
    h = _rms(x, norm1_w) * (1 + scale1) + shift1
    p = jnp.einsum('bld,de->ble', h, w_in)
    pq, pk, pv, hq, hf_f, hf_b, hi, hg, pft = jnp.split(p, IN_SPLITS, axis=-1)

    q = _rms(pq.reshape(B, L, ATTN_HEADS, HEAD_DIM), q_norm_w)
    k = _rms(pk.reshape(B, L, ATTN_KV_HEADS, HEAD_DIM), k_norm_w)
    v = pv.reshape(B, L, ATTN_KV_HEADS, HEAD_DIM)
    if angles is None:
        kk, vv = k, v
    else:
        q = _axial_rope(q, *angles)
        kk = jnp.concatenate([_axial_rope(k, *angles), ctx_k.astype(k.dtype)], axis=1)
        vv = jnp.concatenate([v, ctx_v.astype(v.dtype)], axis=1)
    o_attn = _attend(q.reshape(B, L, ATTN_KV_HEADS, ATTN_GROUP, HEAD_DIM), kk, vv)

    qh = jax.nn.silu(hq.astype(F32)).reshape(B, L, HG_HEADS, HG_DK)
    vh = hi.astype(F32).reshape(B, L, HG_HEADS, HG_DV)
    logf_f, k_f = _forget_gate(hf_f, lb[0])
    logf_b, k_b = _forget_gate(hf_b, lb[1])
    o_f, s_f = _chunk_gla(qh, k_f, vh, logf_f, s_f0.astype(F32))
    o_b, s_b = _chunk_gla(jnp.flip(qh, 1), jnp.flip(k_b, 1), jnp.flip(vh, 1), jnp.flip(logf_b, 1), s_b0.astype(F32))
    o_h = o_f + jnp.flip(o_b, 1)
    o_h = (_rms(o_h, hg_norm_w) * jax.nn.silu(hg.astype(F32).reshape(B, L, HG_HEADS, HG_DV)))
    o_h = o_h.reshape(B, L, HG_WIDTH).astype(x.dtype)

    xf = pft.astype(F32).reshape(B, L, FT_GROUPS, FT_CH)
    mixed = jnp.real(jnp.fft.fft2(xf, axes=(1, 3), norm='ortho'))
    o_ft = jnp.einsum('blgc,gce->blge', mixed, w_ft.astype(F32)).reshape(B, L, FT_WIDTH).astype(x.dtype)

    mix = jnp.concatenate([o_attn, o_h, o_ft], axis=-1)
    x = x + gate1 * jnp.einsum('blm,md->bld', mix, w_out)
    h2 = _rms(x, norm2_w) * (1 + scale2) + shift2
    x = x + gate2 * _hier_moe(h2, w_gr, b_gr, w_er, b_er, w_gate, w_up, w_down)
    return x, k, v, s_f, s_b


def setup_inputs(seed: int = 0) -> dict:
    key = jax.random.key(seed)
    ks = jax.random.split(key, 32)
    nrm = jax.random.normal
    D = D_MODEL
    return {
        'x_prompt': nrm(ks[0], (BATCH, SEQ, D), F32),
        'x_sample': nrm(ks[1], (DEC_BATCH, DEC_SEQ, D), F32),
        'cache_k': nrm(ks[2], (DEC_BATCH, DEPTH, PAST_LEN, ATTN_KV_HEADS, HEAD_DIM), F32),
        'cache_v': nrm(ks[3], (DEC_BATCH, DEPTH, PAST_LEN, ATTN_KV_HEADS, HEAD_DIM), F32),
        'state_fwd': 0.5 * nrm(ks[4], (DEC_BATCH, DEPTH, HG_HEADS, HG_DK, HG_DV), F32),
        'state_bwd': 0.5 * nrm(ks[5], (DEC_BATCH, DEPTH, HG_HEADS, HG_DK, HG_DV), F32),
        'c': nrm(ks[6], (DEC_BATCH, D), F32),
        'c_ctx': nrm(ks[7], (D,), F32),
        'w_ada': 0.5 * D ** -0.5 * nrm(ks[8], (DEPTH, D, 6 * D), F32),
        'b_ada': 0.01 * nrm(ks[9], (DEPTH, 6 * D), F32),
        'norm1_w': 1.0 + 0.02 * nrm(ks[10], (DEPTH, D), F32),
        'norm2_w': 1.0 + 0.02 * nrm(ks[11], (DEPTH, D), F32),
        'w_in': D ** -0.5 * nrm(ks[12], (DEPTH, D, IN_WIDTH), F32),
        'q_norm_w': 1.0 + 0.02 * nrm(ks[13], (DEPTH, HEAD_DIM), F32),
        'k_norm_w': 1.0 + 0.02 * nrm(ks[14], (DEPTH, HEAD_DIM), F32),
        'hg_lb': 0.5 * nrm(ks[15], (DEPTH, 2, HG_WIDTH), F32),
        'hg_norm_w': 1.0 + 0.02 * nrm(ks[16], (DEPTH, HG_DV), F32),
        'w_ft': FT_CH ** -0.5 * nrm(ks[17], (DEPTH, FT_GROUPS, FT_CH, FT_CH), F32),
        'w_out': MIX_WIDTH ** -0.5 * nrm(ks[18], (DEPTH, MIX_WIDTH, D), F32),
        'w_gr': D ** -0.5 * nrm(ks[19], (DEPTH, D, N_GROUPS), F32),
        'b_gr': 0.01 * nrm(ks[20], (DEPTH, N_GROUPS), F32),
        'w_er': D ** -0.5 * nrm(ks[21], (DEPTH, D, N_EXPERTS), F32),
        'b_er': 0.01 * nrm(ks[22], (DEPTH, N_EXPERTS), F32),
        'w_gate': D ** -0.5 * nrm(ks[23], (DEPTH, N_EXPERTS, D, D_EXPERT), F32),
        'w_up': D ** -0.5 * nrm(ks[24], (DEPTH, N_EXPERTS, D, D_EXPERT), F32),
        'w_down': D_EXPERT ** -0.5 * nrm(ks[25], (DEPTH, N_EXPERTS, D_EXPERT, D), F32),
    }


def reference(x_prompt, x_sample, cache_k, cache_v, state_fwd, state_bwd, c, c_ctx,
              w_ada, b_ada, norm1_w, norm2_w, w_in, q_norm_w, k_norm_w, hg_lb, hg_norm_w,
              w_ft, w_out, w_gr, b_gr, w_er, b_er, w_gate, w_up, w_down):
    lb_soft = jax.nn.softmax(hg_lb.astype(F32), axis=0)
    lower_bounds = jnp.cumsum(lb_soft, axis=0) - lb_soft[0]
    angles = _axial_angles(x_sample.shape[1])
    zero_state = jnp.zeros((x_prompt.shape[0], HG_HEADS, HG_DK, HG_DV), F32)
    hp, hs = x_prompt, x_sample
    ks, vs, sfs, sbs = [], [], [], []
    for l in range(DEPTH):
        lw = (norm1_w[l], norm2_w[l], w_in[l], q_norm_w[l], k_norm_w[l], hg_norm_w[l], w_ft[l], w_out[l],
              w_gr[l], b_gr[l], w_er[l], b_er[l], w_gate[l], w_up[l], w_down[l])
        mods_ctx = _modulation(c_ctx[None, :], w_ada[l], b_ada[l])
        hp, k_l, v_l, sf_l, sb_l = _block(hp, mods_ctx, None, None, None, zero_state, zero_state,
                                          lower_bounds[l], *lw)
        ks.append(k_l)
        vs.append(v_l)
        sfs.append(sf_l)
        sbs.append(sb_l)
        mods_lat = _modulation(c, w_ada[l], b_ada[l])
        hs, _, _, _, _ = _block(hs, mods_lat, angles, cache_k[:, l], cache_v[:, l], state_fwd[:, l],
                                state_bwd[:, l], lower_bounds[l], *lw)
    new_cache_k = jnp.stack(ks, axis=1).astype(x_prompt.dtype)
    new_cache_v = jnp.stack(vs, axis=1).astype(x_prompt.dtype)
    new_state_fwd = jnp.stack(sfs, axis=1).astype(x_prompt.dtype)
    new_state_bwd = jnp.stack(sbs, axis=1).astype(x_prompt.dtype)
    return (hp, hs, new_cache_k, new_cache_v, new_state_fwd, new_state_bwd)
```

```python
import functools

import numpy as np
import jax
import jax.numpy as jnp
from jax import lax
from jax.experimental import pallas as pl
from jax.experimental.pallas import tpu as pltpu

F32 = jnp.float32
BF16 = jnp.bfloat16

D_MODEL = 1024
GRID_W = 64
HEAD_DIM = 64
ATTN_WIDTH = D_MODEL // 2
ATTN_HEADS = ATTN_WIDTH // HEAD_DIM
ATTN_KV_HEADS = ATTN_HEADS // 4
KV_WIDTH = ATTN_KV_HEADS * HEAD_DIM
HG_WIDTH = D_MODEL // 4
HG_DK = 64
HG_DK_SHIFT = 6
HG_HEADS = HG_WIDTH // HG_DK
FT_WIDTH = D_MODEL // 4
FT_CH = 64
FT_GROUPS = FT_WIDTH // FT_CH
MIX_WIDTH = ATTN_WIDTH + HG_WIDTH + FT_WIDTH
QK_WIDTH = ATTN_WIDTH + KV_WIDTH
REST_OFF = ATTN_WIDTH + 2 * KV_WIDTH
REST_WIDTH = 5 * HG_WIDTH
FT_OFF = REST_OFF + REST_WIDTH
IN_WIDTH = FT_OFF + FT_WIDTH
N_GROUPS = 4
EXPERTS_PER_GROUP = 4
N_EXPERTS = N_GROUPS * EXPERTS_PER_GROUP
D_EXPERT = 256
GROUP_HIDDEN = EXPERTS_PER_GROUP * D_EXPERT
ROPE_THETA = 10000.0
EPS = 1e-6
EXP_CLIP = 80.0

CHUNK = 64
N_LEVELS = 6
COND_ROWS = 16
ROUTER_LANES = 128
V7X_LANES = 128
V7X_VMEM_BYTES = 64 * 1024 * 1024
VMEM_LIMIT = V7X_VMEM_BYTES * 7 // 8


def _params(*sem):
    return pltpu.CompilerParams(dimension_semantics=sem, vmem_limit_bytes=VMEM_LIMIT)


def _dot(a, b):
    return jnp.dot(a, b, preferred_element_type=F32)


def _dot_nt(a, b):
    return lax.dot_general(a, b, (((1,), (1,)), ((), ())), preferred_element_type=F32)


def _dot_tn(a, b):
    return lax.dot_general(a, b, (((0,), (0,)), ((), ())), preferred_element_type=F32)


def _split2(a):
    hi = a.astype(BF16)
    return hi, (a - hi.astype(F32)).astype(BF16)


def _split3(a):
    hi = a.astype(BF16)
    r = a - hi.astype(F32)
    mid = r.astype(BF16)
    return hi, mid, (r - mid.astype(F32)).astype(BF16)


def _dot_f32(a, b):
    ah, al = _split2(a)
    bh, bl = _split2(b)
    return _dot(ah, bh) + (_dot(ah, bl) + _dot(al, bh))


def _dot_exact_rhs(a, b_bf16):
    ah, al = _split2(a)
    return _dot(ah, b_bf16) + _dot(al, b_bf16)


def _silu(x):
    return x / (1.0 + jnp.exp(-x))


def _rms_rows(x):
    return x * lax.rsqrt(jnp.mean(x * x, axis=-1, keepdims=True) + EPS)


def _mod_kernel(c_ref, w_ref, b_ref, o_ref):
    o_ref[0] = _dot_f32(_silu(c_ref[...]), w_ref[0]) + b_ref[0]


def _modulation(cond, w_ada, b_ada):
    depth, d, n = w_ada.shape
    tn = n // 4
    return pl.pallas_call(
        _mod_kernel,
        out_shape=jax.ShapeDtypeStruct((depth, COND_ROWS, n), F32),
        grid=(depth, n // tn),
        in_specs=[
            pl.BlockSpec((COND_ROWS, d), lambda l, j: (0, 0)),
            pl.BlockSpec((1, d, tn), lambda l, j: (l, 0, j)),
            pl.BlockSpec((1, 1, tn), lambda l, j: (l, 0, j)),
        ],
        out_specs=pl.BlockSpec((1, COND_ROWS, tn), lambda l, j: (l, 0, j)),
        compiler_params=_params("arbitrary", "arbitrary"),
        name="modulation",
    )(cond, w_ada, b_ada.reshape(depth, 1, n))


def _swap_halves16(x):
    lane = lax.broadcasted_iota(jnp.int32, x.shape, 1)
    up = pltpu.roll(x, V7X_LANES - 16, 1)
    down = pltpu.roll(x, 16, 1)
    return jnp.where((lane & 31) < 16, up, down)


def _inproj_kernel(x_ref, mod_ref, n1_ref, w_ref, bdm_ref, nw_ref, cos_ref, sin_ref, ftm_ref,
                   q_ref, kk_ref, v_ref, kvf_ref, rest_ref, xcs_ref):
    h = _rms_rows(x_ref[...]) * n1_ref[...]
    h = h * (1.0 + mod_ref[1:2, :]) + mod_ref[0:1, :]
    p = _dot(h.astype(BF16), w_ref[...])
    qk = p[:, :QK_WIDTH]
    msq = _dot_exact_rhs(qk * qk, bdm_ref[...])
    qkn = qk * lax.rsqrt(msq + EPS) * nw_ref[...]
    cos = cos_ref[...]
    sin = sin_ref[...]
    for j in range(QK_WIDTH // V7X_LANES):
        blk = qkn[:, j * V7X_LANES:(j + 1) * V7X_LANES]
        rot = blk * cos + _swap_halves16(blk) * sin
        if j < ATTN_WIDTH // V7X_LANES:
            q_ref[j] = (rot * HEAD_DIM ** -0.5).astype(BF16)
        else:
            kk_ref[...] = rot.astype(BF16)
    vv = p[:, QK_WIDTH:REST_OFF]
    v_ref[...] = vv.astype(BF16)
    kvf_ref[:, :KV_WIDTH] = qkn[:, ATTN_WIDTH:]
    kvf_ref[:, KV_WIDTH:] = vv
    rest_ref[...] = p[:, REST_OFF:FT_OFF]
    xcs_ref[...] = _dot_f32(p[:, FT_OFF:], ftm_ref[...]).astype(BF16)


def _in_proj(x2d, mods, row_fn, n1, w_in, bdm, nw, cos, sin, pos_fn, ftm, tm):
    t, d = x2d.shape
    full = lambda shape: pl.BlockSpec(shape, lambda i: (0,) * len(shape))
    return pl.pallas_call(
        _inproj_kernel,
        out_shape=[
            jax.ShapeDtypeStruct((ATTN_WIDTH // V7X_LANES, t, V7X_LANES), BF16),
            jax.ShapeDtypeStruct((t, KV_WIDTH), BF16),
            jax.ShapeDtypeStruct((t, KV_WIDTH), BF16),
            jax.ShapeDtypeStruct((t, 2 * KV_WIDTH), F32),
            jax.ShapeDtypeStruct((t, REST_WIDTH), F32),
            jax.ShapeDtypeStruct((t, 2 * FT_WIDTH), BF16),
        ],
        grid=(t // tm,),
        in_specs=[
            pl.BlockSpec((tm, d), lambda i: (i, 0)),
            pl.BlockSpec((None, 6, d), lambda i: (row_fn(i), 0, 0)),
            full((1, d)),
            full((d, IN_WIDTH)),
            full((QK_WIDTH, QK_WIDTH)),
            full((1, QK_WIDTH)),
            pl.BlockSpec((tm, V7X_LANES), lambda i: (pos_fn(i), 0)),
            pl.BlockSpec((tm, V7X_LANES), lambda i: (pos_fn(i), 0)),
            full((FT_WIDTH, 2 * FT_WIDTH)),
        ],
        out_specs=[
            pl.BlockSpec((ATTN_WIDTH // V7X_LANES, tm, V7X_LANES), lambda i: (0, i, 0)),
            pl.BlockSpec((tm, KV_WIDTH), lambda i: (i, 0)),
            pl.BlockSpec((tm, KV_WIDTH), lambda i: (i, 0)),
            pl.BlockSpec((tm, 2 * KV_WIDTH), lambda i: (i, 0)),
            pl.BlockSpec((tm, REST_WIDTH), lambda i: (i, 0)),
            pl.BlockSpec((tm, 2 * FT_WIDTH), lambda i: (i, 0)),
        ],
        compiler_params=_params("arbitrary"),
        name="in_proj",
    )(x2d, mods, n1, w_in, bdm, nw, cos, sin, ftm)


def _attn_kernel(q_ref, kl_ref, kr_ref, vl_ref, vr_ref, o_ref):
    q = q_ref[...]

    def head(k_ref, v_ref):
        s = _dot_nt(q, k_ref[...])
        p = jnp.exp(s - jnp.max(s, axis=-1, keepdims=True))
        l = jnp.sum(p, axis=-1, keepdims=True)
        return _dot(p.astype(BF16), v_ref[...]) * (1.0 / l)

    o_ref[...] = (head(kl_ref, vl_ref) + head(kr_ref, vr_ref)).astype(o_ref.dtype)


def _attention(q4, kl, kr, vl, vr, tq):
    n_pairs, b, l, _ = q4.shape
    lk = kl.shape[2]
    pairs_per_kv = n_pairs // ATTN_KV_HEADS
    kv_spec = pl.BlockSpec((None, None, lk, V7X_LANES), lambda bi, p, i: (bi, p // pairs_per_kv, 0, 0))
    return pl.pallas_call(
        _attn_kernel,
        out_shape=jax.ShapeDtypeStruct((b, l, ATTN_WIDTH), BF16),
        grid=(b, n_pairs, l // tq),
        in_specs=[pl.BlockSpec((None, None, tq, V7X_LANES), lambda bi, p, i: (p, bi, i, 0)),
                  kv_spec, kv_spec, kv_spec, kv_spec],
        out_specs=pl.BlockSpec((None, tq, V7X_LANES), lambda bi, p, i: (bi, i, p)),
        compiler_params=_params("arbitrary", "arbitrary", "arbitrary"),
        name="attention",
    )(q4, kl, kr, vl, vr)


def _kv_layouts(kk, v, b, l, ctx_k, ctx_v):
    def one(a, ctx):
        a = a.reshape(b, l, ATTN_KV_HEADS, HEAD_DIM)
        if ctx is not None:
            a = jnp.concatenate([a, ctx.astype(BF16)], axis=1)
        a = a.transpose(0, 2, 1, 3)
        z = jnp.zeros_like(a)
        return jnp.concatenate([a, z], axis=-1), jnp.concatenate([z, a], axis=-1)

    kl, kr = one(kk, ctx_k)
    vl, vr = one(v, ctx_v)
    return kl, kr, vl, vr


def _gla_constants(reverse):
    c = CHUNK
    t = np.arange(c)[:, None]
    u = np.arange(c)[None, :]
    after = (u >= t) if reverse else (u <= t)
    before = ~after
    blocks = [after, before]
    masks = [np.eye(c, dtype=bool)]
    m = c // 2
    while m >= 1:
        later_t = ((t // m) % 2 == 0) if reverse else ((t // m) % 2 == 1)
        later_u = ((u // m) % 2 == 0) if reverse else ((u // m) % 2 == 1)
        same = (t // m) == (u // m)
        blocks.append(same & np.where(later_t, after, before))
        masks.append(((t // (2 * m)) == (u // (2 * m))) & later_t & ~later_u)
        m //= 2
    mst = np.concatenate(blocks, axis=0).astype(np.float32)
    mst3 = np.concatenate([mst, mst, mst], axis=1)
    msk = np.stack([np.tile(mk, (HG_HEADS, 1)) for mk in masks]).astype(np.float32)
    return mst3, msk


def _gla_chunk(qs, z, v, lb, s_ref, mst, mask_ref, ones3):
    c, w = qs.shape
    logf = jnp.minimum(z, 0.0) - jnp.log1p(jnp.exp(-jnp.abs(z))) + jnp.log1p(lb * jnp.exp(jnp.minimum(-z, EXP_CLIP)))
    kg = (1.0 - lb) / (1.0 + jnp.exp(z))
    l3 = jnp.concatenate(_split3(logf), axis=0)
    ex = jnp.exp(_dot(mst, l3))
    s_old = s_ref[...]
    out = _dot((qs * ex[0:c]).astype(BF16), s_old.astype(BF16))

    lane = lax.broadcasted_iota(jnp.int32, (c, w), 1) >> HG_DK_SHIFT
    vb = v.astype(BF16)
    a = jnp.zeros((HG_HEADS * c, c), F32)
    for lvl in range(N_LEVELS + 1):
        if lvl == 0:
            qm, km = qs, kg
        else:
            e = ex[(1 + lvl) * c:(2 + lvl) * c]
            qm, km = qs * e, kg * e
        qb = qm.astype(BF16)
        lhs = jnp.concatenate([jnp.where(lane == hh, qb, jnp.zeros_like(qb)) for hh in range(HG_HEADS)], axis=0)
        a = a + _dot_nt(lhs, km.astype(BF16)) * mask_ref[lvl]
    o_st = _dot(a.astype(BF16), vb)
    for hh in range(HG_HEADS):
        out = out + jnp.where(lane == hh, o_st[hh * c:(hh + 1) * c], 0.0)

    khat = (kg * ex[c:2 * c]).astype(BF16)
    dcol = jnp.exp(_dot_tn(l3, ones3))
    row_h = lax.broadcasted_iota(jnp.int32, (w, w), 0) >> HG_DK_SHIFT
    col_h = lax.broadcasted_iota(jnp.int32, (w, w), 1) >> HG_DK_SHIFT
    kv = _dot_tn(khat, vb)
    s_ref[...] = jnp.concatenate([dcol, dcol], axis=1) * s_old + jnp.where(row_h == col_h, kv, 0.0)
    return out


def _gla_kernel(rf_ref, rb_ref, lb_ref, s0f_ref, s0b_ref, mstf_ref, mstb_ref, mkf_ref, mkb_ref,
                of_ref, ob_ref, sf_ref, sb_ref, sf_scr, sb_scr):
    i = pl.program_id(1)
    n_chunks = rf_ref.shape[0] // CHUNK
    w = HG_WIDTH

    @pl.when(i == 0)
    def _():
        sf_scr[...] = s0f_ref[...]
        sb_scr[...] = s0b_ref[...]

    ones3 = jnp.ones((3 * CHUNK, V7X_LANES), BF16)

    def body(ci, carry):
        rf = pl.multiple_of(ci * CHUNK, CHUNK)
        rb = pl.multiple_of((n_chunks - 1 - ci) * CHUNK, CHUNK)
        of_ref[pl.ds(rf, CHUNK), :] = _gla_chunk(
            _silu(rf_ref[pl.ds(rf, CHUNK), 0:w]), rf_ref[pl.ds(rf, CHUNK), w:2 * w],
            rf_ref[pl.ds(rf, CHUNK), 3 * w:4 * w], lb_ref[0:1, :], sf_scr, mstf_ref[...], mkf_ref, ones3)
        ob_ref[pl.ds(rb, CHUNK), :] = _gla_chunk(
            _silu(rb_ref[pl.ds(rb, CHUNK), 0:w]), rb_ref[pl.ds(rb, CHUNK), 2 * w:3 * w],
            rb_ref[pl.ds(rb, CHUNK), 3 * w:4 * w], lb_ref[1:2, :], sb_scr, mstb_ref[...], mkb_ref, ones3)
        return carry

    lax.fori_loop(0, n_chunks, body, 0)

    @pl.when(i == pl.num_programs(1) - 1)
    def _():
        sf_ref[...] = sf_scr[...]
        sb_ref[...] = sb_scr[...]


def _gla(rest, lb, s0f, s0b, consts, b, l, tb):
    nb = l // tb
    mstf, mkf, mstb, mkb = consts
    w = HG_WIDTH
    full = lambda a: pl.BlockSpec(a.shape, lambda bi, i: (0,) * a.ndim)
    st_spec = pl.BlockSpec((None, w, w), lambda bi, i: (bi, 0, 0))
    return pl.pallas_call(
        _gla_kernel,
        out_shape=[jax.ShapeDtypeStruct((b * l, w), F32), jax.ShapeDtypeStruct((b * l, w), F32),
                   jax.ShapeDtypeStruct((b, w, w), F32), jax.ShapeDtypeStruct((b, w, w), F32)],
        grid=(b, nb),
        in_specs=[
            pl.BlockSpec((tb, 4 * w), lambda bi, i: (bi * nb + i, 0)),
            pl.BlockSpec((tb, 4 * w), lambda bi, i: (bi * nb + nb - 1 - i, 0)),
            full(lb), st_spec, st_spec, full(mstf), full(mstb), full(mkf), full(mkb),
        ],
        out_specs=[
            pl.BlockSpec((tb, w), lambda bi, i: (bi * nb + i, 0)),
            pl.BlockSpec((tb, w), lambda bi, i: (bi * nb + nb - 1 - i, 0)),
            st_spec, st_spec,
        ],
        scratch_shapes=[pltpu.VMEM((w, w), F32), pltpu.VMEM((w, w), F32)],
        compiler_params=_params("arbitrary", "arbitrary"),
        name="gla",
    )(rest, rest, lb, s0f, s0b, mstf, mstb, mkf, mkb)


def _fat_state(s):
    b = s.shape[0]
    eye = jnp.eye(HG_HEADS, dtype=s.dtype)
    return jnp.einsum("bhkv,hg->bhkgv", s, eye).reshape(b, HG_WIDTH, HG_WIDTH)


def _thin_state(s):
    return jnp.stack([s[:, h * HG_DK:(h + 1) * HG_DK, h * HG_DK:(h + 1) * HG_DK] for h in range(HG_HEADS)], axis=1)


def _fnet_kernel(c_ref, s_ref, x_ref, w_ref, o_ref):
    y = _dot(c_ref[...], x_ref[:, :FT_WIDTH]) + _dot(s_ref[...], x_ref[:, FT_WIDTH:])
    o_ref[...] = _dot(y.astype(BF16), w_ref[...]).astype(o_ref.dtype)


def _fnet(xcs, cos_t, sin_t, w_bd, b, l, tl):
    return pl.pallas_call(
        _fnet_kernel,
        out_shape=jax.ShapeDtypeStruct((b, l, FT_WIDTH), BF16),
        grid=(l // tl, b),
        in_specs=[
            pl.BlockSpec((tl, l), lambda i, bi: (i, 0)),
            pl.BlockSpec((tl, l), lambda i, bi: (i, 0)),
            pl.BlockSpec((None, l, 2 * FT_WIDTH), lambda i, bi: (bi, 0, 0)),
            pl.BlockSpec((FT_WIDTH, FT_WIDTH), lambda i, bi: (0, 0)),
        ],
        out_specs=pl.BlockSpec((None, tl, FT_WIDTH), lambda i, bi: (bi, i, 0)),
        compiler_params=_params("arbitrary", "arbitrary"),
        name="fnet",
    )(cos_t, sin_t, xcs.reshape(b, l, 2 * FT_WIDTH), w_bd)


def _dft_tables(l):
    idx = jnp.arange(l, dtype=jnp.int32)
    ang = ((idx[:, None] * idx[None, :]) % l).astype(F32) * (2.0 * np.pi / l)
    scale = 1.0 / np.sqrt(l * FT_CH)
    return (jnp.cos(ang) * scale).astype(BF16), (-jnp.sin(ang) * scale).astype(BF16)


def _channel_dft():
    idx = np.arange(FT_CH)
    ang = 2.0 * np.pi * ((idx[:, None] * idx[None, :]) % FT_CH) / FT_CH
    eye = np.eye(FT_GROUPS)
    return np.concatenate([np.kron(eye, np.cos(ang)), np.kron(eye, np.sin(ang))], axis=1).astype(np.float32)


def _route(r):
    lane = lax.broadcasted_iota(jnp.int32, r.shape, 1).astype(F32)
    neg = -jnp.inf
    big = float(ROUTER_LANES)
    is_g = lane < N_GROUPS
    lg = jnp.where(is_g, r, neg)
    g_max = jnp.max(lg, axis=-1, keepdims=True)
    g_idx = jnp.min(jnp.where(lg == g_max, lane, big), axis=-1, keepdims=True)
    g_w = 1.0 / jnp.sum(jnp.where(is_g, jnp.exp(lg - g_max), 0.0), axis=-1, keepdims=True)
    e_lo = N_GROUPS + g_idx * EXPERTS_PER_GROUP
    in_g = (lane >= e_lo) & (lane < e_lo + EXPERTS_PER_GROUP)
    le = jnp.where(in_g, r, neg)
    v1 = jnp.max(le, axis=-1, keepdims=True)
    i1 = jnp.min(jnp.where(in_g & (le == v1), lane, big), axis=-1, keepdims=True)
    le2 = jnp.where(lane == i1, neg, le)
    v2 = jnp.max(le2, axis=-1, keepdims=True)
    i2 = jnp.min(jnp.where(in_g & (lane != i1) & (le2 == v2), lane, big), axis=-1, keepdims=True)
    e2 = jnp.exp(v2 - v1)
    den = 1.0 + e2
    return jnp.where(lane == i1, g_w / den, jnp.where(lane == i2, g_w * e2 / den, 0.0))


def _outproj_kernel(x_ref, oa_ref, of_ref, ob_ref, hg_ref, oft_ref, mod_ref, hgw_ref, bdm_ref, wo_ref,
                    n2_ref, wr_ref, br_ref, x1_ref, h2_ref, gate_ref):
    oh = of_ref[...] + ob_ref[...]
    oh = oh * lax.rsqrt(_dot_exact_rhs(oh * oh, bdm_ref[...]) + EPS) * hgw_ref[...] * _silu(hg_ref[...])
    acc = _dot(oa_ref[...], wo_ref[:ATTN_WIDTH, :])
    acc = acc + _dot(oh.astype(BF16), wo_ref[ATTN_WIDTH:ATTN_WIDTH + HG_WIDTH, :])
    acc = acc + _dot(oft_ref[...], wo_ref[ATTN_WIDTH + HG_WIDTH:, :])
    x1 = x_ref[...] + mod_ref[2:3, :] * acc
    x1_ref[...] = x1
    h2 = _rms_rows(x1) * n2_ref[...]
    h2 = h2 * (1.0 + mod_ref[4:5, :]) + mod_ref[3:4, :]
    h2_ref[...] = h2.astype(BF16)
    gate_ref[...] = _route(_dot_f32(h2, wr_ref[...]) + br_ref[...])


def _out_proj(x2d, oa, of, ob, rest, oft, mods, row_fn, hgw, bdm, wo, n2, wr, br, tm):
    t, d = x2d.shape
    full = lambda shape: pl.BlockSpec(shape, lambda i: (0,) * len(shape))
    rows = lambda width: pl.BlockSpec((tm, width), lambda i: (i, 0))
    return pl.pallas_call(
        _outproj_kernel,
        out_shape=[jax.ShapeDtypeStruct((t, d), F32), jax.ShapeDtypeStruct((t, d), BF16),
                   jax.ShapeDtypeStruct((t, ROUTER_LANES), F32)],
        grid=(t // tm,),
        in_specs=[
            rows(d), rows(ATTN_WIDTH), rows(HG_WIDTH), rows(HG_WIDTH),
            pl.BlockSpec((tm, HG_WIDTH), lambda i: (i, REST_WIDTH // HG_WIDTH - 1)),
            rows(FT_WIDTH),
            pl.BlockSpec((None, 6, d), lambda i: (row_fn(i), 0, 0)),
            full((1, HG_WIDTH)), full((HG_WIDTH, HG_WIDTH)), full((MIX_WIDTH, d)), full((1, d)),
            full((d, ROUTER_LANES)), full((1, ROUTER_LANES)),
        ],
        out_specs=[rows(d), rows(d), rows(ROUTER_LANES)],
        compiler_params=_params("arbitrary"),
        name="out_proj",
    )(x2d, oa, of, ob, rest, oft, mods, hgw, bdm, wo, n2, wr, br)


def _moe_kernel(h_ref, gate_ref, x1_ref, mod_ref, wgu_ref, wd_ref, y_ref):
    h = h_ref[...]
    g = gate_ref[...]
    tm = h.shape[0]
    acc = jnp.zeros((tm, D_MODEL), F32)
    for grp in range(N_GROUPS):
        au = _dot(h, wgu_ref[grp])
        lane0 = N_GROUPS + grp * EXPERTS_PER_GROUP
        gexp = jnp.concatenate(
            [jnp.broadcast_to(g[:, lane0 + e:lane0 + e + 1], (tm, D_EXPERT)) for e in range(EXPERTS_PER_GROUP)],
            axis=1)
        hid = _silu(au[:, :GROUP_HIDDEN]) * au[:, GROUP_HIDDEN:] * gexp
        acc = acc + _dot(hid.astype(BF16), wd_ref[grp])
    y_ref[...] = x1_ref[...] + mod_ref[5:6, :] * acc


def _moe(h2, gate, x1, mods, row_fn, wgu, wd, tm):
    t, d = x1.shape
    rows = lambda width: pl.BlockSpec((tm, width), lambda i: (i, 0))
    resident = lambda a: pl.BlockSpec(a.shape, lambda i: (0,) * a.ndim, pipeline_mode=pl.Buffered(1))
    return pl.pallas_call(
        _moe_kernel,
        out_shape=jax.ShapeDtypeStruct((t, d), F32),
        grid=(t // tm,),
        in_specs=[rows(d), rows(ROUTER_LANES), rows(d),
                  pl.BlockSpec((None, 6, d), lambda i: (row_fn(i), 0, 0)),
                  resident(wgu), resident(wd)],
        out_specs=rows(d),
        compiler_params=_params("arbitrary"),
        name="moe",
    )(h2, gate, x1, mods, wgu, wd)


def _block_diag_mean(n_heads, width):
    return np.kron(np.eye(n_heads), np.full((width, width), 1.0 / width)).astype(np.float32)


def _rope_tables(n_tokens):
    t = jnp.arange(n_tokens)
    row = (t // GRID_W).astype(F32)
    col = (t % GRID_W).astype(F32)
    half = HEAD_DIM // 2
    inv = 1.0 / (ROPE_THETA ** (jnp.arange(0, half, 2, dtype=F32) / half))
    ar = row[:, None] * inv[None, :]
    ac = col[:, None] * inv[None, :]
    cos = jnp.concatenate([jnp.cos(ar), jnp.cos(ar), jnp.cos(ac), jnp.cos(ac)], axis=-1)
    sin = jnp.concatenate([-jnp.sin(ar), jnp.sin(ar), -jnp.sin(ac), jnp.sin(ac)], axis=-1)
    reps = V7X_LANES // HEAD_DIM
    return jnp.tile(cos, (1, reps)), jnp.tile(sin, (1, reps))


def _layer_weights(l, norm1_w, norm2_w, w_in, q_norm_w, k_norm_w, hg_norm_w, w_ft, w_out,
                   w_gr, b_gr, w_er, b_er, w_gate, w_up, w_down):
    d = D_MODEL
    nw = jnp.concatenate([jnp.tile(q_norm_w[l], ATTN_HEADS), jnp.tile(k_norm_w[l], ATTN_KV_HEADS)])[None, :]
    eye = jnp.eye(FT_GROUPS, dtype=F32)
    w_ft_bd = jnp.einsum("gce,gh->gche", w_ft[l].astype(F32), eye).reshape(FT_WIDTH, FT_WIDTH).astype(BF16)
    pad = ROUTER_LANES - N_GROUPS - N_EXPERTS
    wr = jnp.concatenate([w_gr[l], w_er[l], jnp.zeros((d, pad), F32)], axis=1)
    br = jnp.concatenate([b_gr[l], b_er[l], jnp.zeros((pad,), F32)])[None, :]

    def grouped(w):
        return w.reshape(N_GROUPS, EXPERTS_PER_GROUP, d, D_EXPERT).transpose(0, 2, 1, 3).reshape(
            N_GROUPS, d, GROUP_HIDDEN)

    wgu = jnp.concatenate([grouped(w_gate[l]), grouped(w_up[l])], axis=-1).astype(BF16)
    wd = w_down[l].reshape(N_GROUPS, GROUP_HIDDEN, d).astype(BF16)
    return dict(n1=norm1_w[l][None, :], n2=norm2_w[l][None, :], w_in=w_in[l].astype(BF16), nw=nw,
                hgw=jnp.tile(hg_norm_w[l], HG_HEADS)[None, :], w_ft=w_ft_bd, w_out=w_out[l].astype(BF16),
                wr=wr, br=br, wgu=wgu, wd=wd)


def _run_block(x, mods, row_fn, rope, pos_fn, ctx_k, ctx_v, s0f, s0b, lb, lw, consts, tiles):
    b, l, d = x.shape
    tm, tq, tb, tl = tiles
    x2d = x.reshape(b * l, d)
    cos, sin = rope
    q4, kk, v, kvf, rest, xcs = _in_proj(x2d, mods, row_fn, lw["n1"], lw["w_in"], consts["bdm_qk"], lw["nw"],
                                         cos, sin, pos_fn, consts["ftm"], tm)
    kl, kr, vl, vr = _kv_layouts(kk, v, b, l, ctx_k, ctx_v)
    o_attn = _attention(q4.reshape(q4.shape[0], b, l, V7X_LANES), kl, kr, vl, vr, tq)
    o_f, o_b, s_f, s_b = _gla(rest, lb, s0f, s0b, consts["gla"], b, l, tb)
    cos_t, sin_t = consts["dft"][l]
    o_ft = _fnet(xcs, cos_t, sin_t, lw["w_ft"], b, l, tl)
    x1, h2, gate = _out_proj(x2d, o_attn.reshape(b * l, ATTN_WIDTH), o_f, o_b, rest,
                             o_ft.reshape(b * l, FT_WIDTH), mods, row_fn, lw["hgw"], consts["bdm_hg"],
                             lw["w_out"], lw["n2"], lw["wr"], lw["br"], tm)
    y = _moe(h2, gate, x1, mods, row_fn, lw["wgu"], lw["wd"], tm)
    return y.reshape(b, l, d), kvf, s_f, s_b


def kernel(x_prompt, x_sample, cache_k, cache_v, state_fwd, state_bwd, c, c_ctx, w_ada, b_ada, norm1_w, norm2_w,
           w_in, q_norm_w, k_norm_w, hg_lb, hg_norm_w, w_ft, w_out, w_gr, b_gr, w_er, b_er, w_gate, w_up, w_down):
    bp, lp, d = x_prompt.shape
    bs, ls, _ = x_sample.shape
    depth = w_in.shape[0]
    assert d == D_MODEL and ls % GRID_W == 0 and bs + 1 <= COND_ROWS

    tm_p = min(512, bp * lp)
    tm_s = min(512, ls)
    tiles_p = (tm_p, min(256, lp), min(512, lp), min(512, lp))
    tiles_s = (tm_s, min(256, ls), min(512, ls), min(512, ls))
    assert (bp * lp) % tm_p == 0 and ls % tm_s == 0
    tiles_per_seq = ls // tm_s

    lb_soft = jax.nn.softmax(hg_lb.astype(F32), axis=0)
    lower_bounds = jnp.cumsum(lb_soft, axis=0) - lb_soft[0]

    cond = jnp.concatenate([c_ctx[None, :], c, jnp.zeros((COND_ROWS - 1 - bs, d), F32)], axis=0)
    mods = _modulation(cond, w_ada, b_ada).reshape(depth, COND_ROWS, 6, d)

    mstf, mkf = _gla_constants(False)
    mstb, mkb = _gla_constants(True)
    consts = dict(
        bdm_qk=jnp.asarray(_block_diag_mean(ATTN_HEADS + ATTN_KV_HEADS, HEAD_DIM), BF16),
        bdm_hg=jnp.asarray(_block_diag_mean(HG_HEADS, HG_DK), BF16),
        ftm=jnp.asarray(_channel_dft()),
        gla=(jnp.asarray(mstf, BF16), jnp.asarray(mkf), jnp.asarray(mstb, BF16), jnp.asarray(mkb)),
        dft={lp: _dft_tables(lp), ls: _dft_tables(ls)},
    )
    rope_s = _rope_tables(ls)
    rope_p = (jnp.ones((tm_p, V7X_LANES), F32), jnp.zeros((tm_p, V7X_LANES), F32))
    zero_state = jnp.zeros((bp, HG_WIDTH, HG_WIDTH), F32)

    hp, hs = x_prompt, x_sample
    ks, vs, sfs, sbs = [], [], [], []
    for l in range(depth):
        lw = _layer_weights(l, norm1_w, norm2_w, w_in, q_norm_w, k_norm_w, hg_norm_w, w_ft, w_out,
                            w_gr, b_gr, w_er, b_er, w_gate, w_up, w_down)
        lb = lower_bounds[l]
        hp, kvf, s_f, s_b = _run_block(hp, mods[l], lambda i: 0, rope_p, lambda i: 0, None, None,
                                       zero_state, zero_state, lb, lw, consts, tiles_p)
        ks.append(kvf[:, :KV_WIDTH].reshape(bp, lp, ATTN_KV_HEADS, HEAD_DIM))
        vs.append(kvf[:, KV_WIDTH:].reshape(bp, lp, ATTN_KV_HEADS, HEAD_DIM))
        sfs.append(_thin_state(s_f))
        sbs.append(_thin_state(s_b))
        hs, _, _, _ = _run_block(hs, mods[l], lambda i: 1 + i // tiles_per_seq, rope_s,
                                 lambda i: i % tiles_per_seq, cache_k[:, l], cache_v[:, l],
                                 _fat_state(state_fwd[:, l].astype(F32)), _fat_state(state_bwd[:, l].astype(F32)),
                                 lb, lw, consts, tiles_s)
    dt = x_prompt.dtype
    return (hp, hs, jnp.stack(ks, axis=1).astype(dt), jnp.stack(vs, axis=1).astype(dt),
            jnp.stack(sfs, axis=1).astype(dt), jnp.stack(sbs, axis=1).astype(dt))
```
